```python
import math, functools
import jax, jax.numpy as jnp
from jax import lax
import numpy as np

D_MODEL = 1024
BATCH = 4
SEQ = 4096
DEPTH = 2
DEC_BATCH = 128
DEC_SEQ = 1
PAST_LEN = 2048
PAGE_SIZE = 128

A_HEADS = 4
A_HEAD_DIM = 64
IDX_HEADS = 8
IDX_DIM = 32
TOPK_MAX = 256
Q_BLOCK = 128
DN_HEADS = 4
DN_HEAD_DIM = 64
DN_CONV = 4
SSM_HEADS = 8
SSM_HEAD_DIM = 64
SSM_GROUPS = 2
SSM_STATE = 128
SSM_CONV = 4
CHUNK = 64
MEM_LEN = 256
X_HEADS = 4
X_HEAD_DIM = 64
D_FF = 11 * D_MODEL // 4
FFN_CONV = 3
EPS = 1e-6

A_W = A_HEADS * A_HEAD_DIM
DN_W = DN_HEADS * DN_HEAD_DIM
SSM_W = SSM_HEADS * SSM_HEAD_DIM
SSM_CONV_CH = SSM_W + 2 * SSM_GROUPS * SSM_STATE
MIX_W = A_W + DN_W + SSM_W
X_W = X_HEADS * X_HEAD_DIM
IN_SIZES = (A_W, A_W, A_W, IDX_HEADS * IDX_DIM, IDX_DIM, IDX_HEADS,
            3 * DN_W, DN_HEADS, DN_HEADS, DN_W,
            SSM_W, SSM_CONV_CH, SSM_HEADS)
N_IN = 3 * A_W + IDX_HEADS * IDX_DIM + IDX_DIM + IDX_HEADS + 4 * DN_W + 2 * DN_HEADS + SSM_W + SSM_CONV_CH + SSM_HEADS

kernel_name = 'hybrid_dsa_deltanet_ssd_step'


def rms_norm(x, g):
    xf = x.astype(jnp.float32)
    y = xf * lax.rsqrt(jnp.mean(xf * xf, axis=-1, keepdims=True) + EPS)
    return (y * g.astype(jnp.float32)).astype(x.dtype)


def l2_normalize(x):
    xf = x.astype(jnp.float32)
    return xf * lax.rsqrt(jnp.sum(xf * xf, axis=-1, keepdims=True) + EPS)


def split_cols(h, sizes):
    cuts, acc = [], 0
    for s in sizes[:-1]:
        acc += s
        cuts.append(acc)
    return jnp.split(h, cuts, axis=-1)


def causal_conv(x, buf, w):
    width, t = w.shape[0], x.shape[1]
    xp = jnp.concatenate([buf.astype(x.dtype), x], axis=1)
    y = xp[:, 0:t] * w[0]
    for j in range(1, width):
        y = y + xp[:, j:j + t] * w[j]
    return y, xp[:, t:]


def chunk_layout(t):
    c = min(CHUNK, t)
    n = -(-t // c)
    return c, n, n * c - t


def pad_time(a, pad):
    return jnp.pad(a, [(0, 0), (0, pad)] + [(0, 0)] * (a.ndim - 2))


def gated_delta_rule(q, k, v, beta, g, s0):
    b, t, h, dk = q.shape
    c, n, pad = chunk_layout(t)
    f32 = jnp.float32

    def prep(a):
        a = pad_time(a.astype(f32), pad)
        return jnp.moveaxis(a.reshape((b, n, c) + a.shape[2:]), 3, 1)

    q, k, v, beta, g = prep(q), prep(k), prep(v), prep(beta), prep(g)
    q = q * dk ** -0.5
    gc = jnp.cumsum(g, axis=-1)
    tril = jnp.tril(jnp.ones((c, c), bool))
    strict = jnp.tril(jnp.ones((c, c), bool), -1)
    diff = gc[..., :, None] - gc[..., None, :]
    decay = jnp.where(tril, jnp.exp(jnp.where(tril, diff, 0.0)), 0.0)
    kb = k * beta[..., None]
    a_mat = jnp.where(strict, jnp.einsum('bhnid,bhnjd->bhnij', kb, k) * decay, 0.0)
    eye = jnp.eye(c, dtype=f32)
    t_inv = lax.linalg.triangular_solve(a_mat + eye, jnp.broadcast_to(eye, a_mat.shape), left_side=True, lower=True)
    u = t_inv @ (v * beta[..., None])
    w = t_inv @ (kb * jnp.exp(gc)[..., None])
    attn = jnp.einsum('bhnid,bhnjd->bhnij', q, k) * decay
    g_last = gc[..., -1]
    k_dec = k * jnp.exp(g_last[..., None] - gc)[..., None]

    def step(s, inp):
        q_i, w_i, u_i, attn_i, kd_i, gl_i, gc_i = inp
        v_new = u_i - jnp.einsum('bhcd,bhde->bhce', w_i, s)
        o = jnp.einsum('bhcd,bhde->bhce', q_i * jnp.exp(gc_i)[..., None], s) + jnp.einsum('bhij,bhje->bhie', attn_i, v_new)
        s = s * jnp.exp(gl_i)[..., None, None] + jnp.einsum('bhcd,bhce->bhde', kd_i, v_new)
        return s, o

    xs = tuple(jnp.moveaxis(a, 2, 0) for a in (q, w, u, attn, k_dec, g_last, gc))
    s_fin, o = lax.scan(step, s0.astype(f32), xs)
    o = jnp.moveaxis(jnp.moveaxis(o, 0, 2), 1, 3).reshape(b, n * c, h, v.shape[-1])[:, :t]
    return o, s_fin


def ssd_scan(x, dt, a_neg, bm, cm, d_skip, h0):
    bsz, t, h, p = x.shape
    gsz, nst = bm.shape[2], bm.shape[3]
    r = h // gsz
    c, n, pad = chunk_layout(t)
    f32 = jnp.float32

    def prep(a):
        a = pad_time(a.astype(f32), pad)
        return a.reshape((bsz, n, c) + a.shape[2:])

    xc = prep(x).reshape(bsz, n, c, gsz, r, p)
    dtc = prep(dt).reshape(bsz, n, c, gsz, r)
    bc, cc = prep(bm), prep(cm)
    acum = jnp.cumsum(dtc * a_neg.astype(f32).reshape(gsz, r), axis=2)
    tril = jnp.tril(jnp.ones((c, c), bool))[None, None, :, :, None, None]
    diff = acum[:, :, :, None] - acum[:, :, None, :]
    lmat = jnp.where(tril, jnp.exp(jnp.where(tril, diff, 0.0)), 0.0)
    cb = jnp.einsum('bnigs,bnjgs->bnijg', cc, bc)
    m = cb[..., None] * lmat * dtc[:, :, None]
    y = jnp.einsum('bnijgr,bnjgrp->bnigrp', m, xc)
    a_last = acum[:, :, -1]
    wdec = jnp.exp(a_last[:, :, None] - acum) * dtc
    chunk_h = jnp.einsum('bnjgr,bnjgs,bnjgrp->bngrps', wdec, bc, xc)

    def step(hs, inp):
        ch, al = inp
        return hs * jnp.exp(al)[..., None, None] + ch, hs

    h_fin, h_prev = lax.scan(step, h0.astype(f32).reshape(bsz, gsz, r, p, nst),
                             (jnp.moveaxis(chunk_h, 1, 0), jnp.moveaxis(a_last, 1, 0)))
    h_prev = jnp.moveaxis(h_prev, 0, 1)
    y = y + jnp.einsum('bnigs,bnigr,bngrps->bnigrp', cc, jnp.exp(acum), h_prev)
    y = y + d_skip.astype(f32).reshape(gsz, r)[..., None] * xc
    y = y.reshape(bsz, n * c, h, p)[:, :t]
    return y.astype(x.dtype), h_fin.reshape(bsz, h, p, nst)


def indexer_topk(qi, wi, ki_all, pos, n_sel):
    f32 = jnp.float32
    dots = jnp.einsum('bthd,bsd->bths', qi.astype(f32), ki_all.astype(f32)) * IDX_DIM ** -0.5
    score = jnp.einsum('bth,bths->bts', wi.astype(f32) * IDX_HEADS ** -0.5, jax.nn.relu(dots))
    visible = jnp.arange(ki_all.shape[1])[None, :] <= pos[:, None]
    score = jnp.where(visible[None], score, -jnp.inf)
    vals, idx = lax.top_k(score, n_sel)
    return idx, jnp.isfinite(vals)


def sparse_attend(q, ks, vs, valid):
    f32 = jnp.float32
    logits = jnp.einsum('bthd,btkhd->bthk', q.astype(f32), ks.astype(f32)) * A_HEAD_DIM ** -0.5
    logits = jnp.where(valid[:, :, None, :], logits, -jnp.inf)
    p = jax.nn.softmax(logits, axis=-1)
    return jnp.einsum('bthk,btkhd->bthd', p, vs.astype(f32)).astype(q.dtype)


def dsa_prompt(q, k, v, qi, ki, wi):
    b, s = q.shape[:2]
    n_sel = min(TOPK_MAX, s // 4)
    nb = s // Q_BLOCK
    take = jax.vmap(lambda a, i: a[i])

    def blk(args):
        qb, qib, wib, pos = args
        idx, valid = indexer_topk(qib, wib, ki, pos, n_sel)
        return sparse_attend(qb, take(k, idx), take(v, idx), valid)

    def to_blocks(a):
        return jnp.moveaxis(a.reshape((b, nb, Q_BLOCK) + a.shape[2:]), 1, 0)

    pos = jnp.arange(s).reshape(nb, Q_BLOCK)
    o = lax.map(blk, (to_blocks(q), to_blocks(qi), to_blocks(wi), pos))
    return jnp.moveaxis(o, 0, 1).reshape(q.shape)


def dsa_sample(q, k, v, qi, ki, wi, k_pool, v_pool, ki_pool, page_table):
    b, t = q.shape[:2]
    past = page_table.shape[1] * PAGE_SIZE
    n_sel = min(TOPK_MAX, (past + t) // 4)
    ki_past = ki_pool[page_table].reshape(b, past, IDX_DIM)
    ki_all = jnp.concatenate([ki_past.astype(ki.dtype), ki], axis=1)
    idx, valid = indexer_topk(qi, wi, ki_all, past + jnp.arange(t), n_sel)
    take = jax.vmap(lambda a, i: a[i])
    pidx = jnp.minimum(idx, past - 1)
    phys = take(page_table, pidx // PAGE_SIZE)
    off = pidx % PAGE_SIZE
    nidx = jnp.clip(idx - past, 0, t - 1)
    in_past = (idx < past)[..., None, None]
    ks = jnp.where(in_past, k_pool[phys, off].astype(k.dtype), take(k, nidx))
    vs = jnp.where(in_past, v_pool[phys, off].astype(v.dtype), take(v, nidx))
    return sparse_attend(q, ks, vs, valid)


def hybrid_layer(x, l, P, attn_fn, mem_k, mem_v, dn_buf, dn_s, ssm_buf, ssm_h, ffn_buf):
    b, t, _ = x.shape
    f32 = jnp.float32
    hn = rms_norm(x, P['g_pre_mix'][l])
    (aq, ak, av, aqi, aki, awi, dqkv, dbeta, dalpha, dgate,
     sz, sxbc, sdt) = split_cols(hn @ P['w_in'][l], IN_SIZES)
    ak = ak.reshape(b, t, A_HEADS, A_HEAD_DIM)
    av = av.reshape(b, t, A_HEADS, A_HEAD_DIM)
    o_a = attn_fn(aq.reshape(b, t, A_HEADS, A_HEAD_DIM), ak, av,
                  aqi.reshape(b, t, IDX_HEADS, IDX_DIM), aki, awi).reshape(b, t, A_W)
    dqkv, dn_buf = causal_conv(dqkv, dn_buf, P['dn_conv_w'][l])
    dq, dk, dv = jnp.split(jax.nn.silu(dqkv), 3, axis=-1)
    hs = (b, t, DN_HEADS, DN_HEAD_DIM)
    beta = jax.nn.sigmoid(dbeta.astype(f32))
    g = -jnp.exp(P['dn_a_log'][l].astype(f32)) * jax.nn.softplus(dalpha.astype(f32) + P['dn_dt_bias'][l].astype(f32))
    o_b, dn_s = gated_delta_rule(l2_normalize(dq.reshape(hs)), l2_normalize(dk.reshape(hs)), dv.reshape(hs), beta, g, dn_s)
    o_b = (rms_norm(o_b.astype(x.dtype), P['dn_norm_g'][l]) * jax.nn.silu(dgate.reshape(hs))).reshape(b, t, DN_W)
    sxbc, ssm_buf = causal_conv(sxbc, ssm_buf, P['ssm_conv_w'][l])
    sxbc = jax.nn.silu(sxbc + P['ssm_conv_b'][l])
    sx, sb, sc = split_cols(sxbc, (SSM_W, SSM_GROUPS * SSM_STATE, SSM_GROUPS * SSM_STATE))
    dt = jax.nn.softplus(sdt.astype(f32) + P['ssm_dt_bias'][l].astype(f32))
    y, ssm_h = ssd_scan(sx.reshape(b, t, SSM_HEADS, SSM_HEAD_DIM), dt, -jnp.exp(P['ssm_a_log'][l].astype(f32)),
                        sb.reshape(b, t, SSM_GROUPS, SSM_STATE), sc.reshape(b, t, SSM_GROUPS, SSM_STATE),
                        P['ssm_d'][l], ssm_h)
    o_c = rms_norm(y.reshape(b, t, SSM_W) * jax.nn.silu(sz), P['ssm_norm_g'][l])
    mix = jnp.concatenate([o_a, o_b, o_c], axis=-1) @ P['w_out'][l]
    x = x + rms_norm(mix, P['g_post_mix'][l])
    hn = rms_norm(x, P['g_pre_x'][l])
    q = (hn @ P['w_xq'][l]).reshape(b, t, X_HEADS, X_HEAD_DIM)
    logits = jnp.einsum('bthd,bmhd->bhtm', q.astype(f32), mem_k.astype(f32)) * X_HEAD_DIM ** -0.5
    o = jnp.einsum('bhtm,bmhd->bthd', jax.nn.softmax(logits, axis=-1), mem_v.astype(f32)).astype(x.dtype)
    x = x + rms_norm(o.reshape(b, t, X_W) @ P['w_xo'][l], P['g_post_x'][l])
    hn = rms_norm(x, P['g_pre_ffn'][l])
    gate, ffn_buf = causal_conv(hn @ P['w_gate'][l], ffn_buf, P['ffn_conv_w'][l])
    x = x + rms_norm((jax.nn.silu(gate) * (hn @ P['w_up'][l])) @ P['w_down'][l], P['g_post_ffn'][l])
    return x, (ak, av, aki, dn_buf, dn_s, ssm_buf, ssm_h, ffn_buf)


def setup_inputs(seed: int = 0) -> dict:
    key = jax.random.key(seed)
    ks = iter(jax.random.split(key, 64))
    f32 = jnp.float32

    def nrm(shape, scale=1.0):
        return jax.random.normal(next(ks), shape, f32) * scale

    def gain(shape):
        return 1.0 + nrm(shape, 0.02)

    def dt_bias(shape):
        u = jax.random.uniform(next(ks), shape, f32)
        dtv = jnp.exp(u * (math.log(0.1) - math.log(0.001)) + math.log(0.001))
        return dtv + jnp.log(-jnp.expm1(-dtv))

    def a_log(shape):
        return jnp.log(jax.random.uniform(next(ks), shape, f32, 1.0, 16.0))

    n_pages = PAST_LEN // PAGE_SIZE
    n_used = DEC_BATCH * n_pages
    n_pool = (5 * n_used + 3) // 4
    page_table = jax.random.permutation(next(ks), n_pool)[:n_used].reshape(DEC_BATCH, n_pages).astype(jnp.int32)
    return {
        'x_prompt': nrm((BATCH, SEQ, D_MODEL)),
        'x_sample': nrm((DEC_BATCH, DEC_SEQ, D_MODEL)),
        'cache_k': nrm((DEPTH, n_pool, PAGE_SIZE, A_HEADS, A_HEAD_DIM)),
        'cache_v': nrm((DEPTH, n_pool, PAGE_SIZE, A_HEADS, A_HEAD_DIM)),
        'cache_idx_k': nrm((DEPTH, n_pool, PAGE_SIZE, IDX_DIM)),
        'state_dn_conv': nrm((DEPTH, DEC_BATCH, DN_CONV - 1, 3 * DN_W)),
        'state_dn': nrm((DEPTH, DEC_BATCH, DN_HEADS, DN_HEAD_DIM, DN_HEAD_DIM), 0.3),
        'state_ssm_conv': nrm((DEPTH, DEC_BATCH, SSM_CONV - 1, SSM_CONV_CH)),
        'state_ssm': nrm((DEPTH, DEC_BATCH, SSM_HEADS, SSM_HEAD_DIM, SSM_STATE), 0.3),
        'state_ffn_conv': nrm((DEPTH, DEC_BATCH, FFN_CONV - 1, D_FF)),
        'cache_mem_k': nrm((DEPTH, DEC_BATCH, MEM_LEN, X_HEADS, X_HEAD_DIM)),
        'cache_mem_v': nrm((DEPTH, DEC_BATCH, MEM_LEN, X_HEADS, X_HEAD_DIM)),
        'page_table': page_table,
        'mem_prompt': nrm((BATCH, MEM_LEN, D_MODEL)),
        'g_pre_mix': gain((DEPTH, D_MODEL)),
        'w_in': nrm((DEPTH, D_MODEL, N_IN), D_MODEL ** -0.5),
        'dn_conv_w': nrm((DEPTH, DN_CONV, 3 * DN_W), DN_CONV ** -0.5),
        'dn_a_log': a_log((DEPTH, DN_HEADS)),
        'dn_dt_bias': dt_bias((DEPTH, DN_HEADS)),
        'dn_norm_g': gain((DEPTH, DN_HEAD_DIM)),
        'ssm_conv_w': nrm((DEPTH, SSM_CONV, SSM_CONV_CH), SSM_CONV ** -0.5),
        'ssm_conv_b': nrm((DEPTH, SSM_CONV_CH), 0.01),
        'ssm_a_log': a_log((DEPTH, SSM_HEADS)),
        'ssm_dt_bias': dt_bias((DEPTH, SSM_HEADS)),
        'ssm_d': 1.0 + nrm((DEPTH, SSM_HEADS), 0.1),
        'ssm_norm_g': gain((DEPTH, SSM_W)),
        'w_out': nrm((DEPTH, MIX_W, D_MODEL), MIX_W ** -0.5),
        'g_post_mix': gain((DEPTH, D_MODEL)),
        'g_pre_x': gain((DEPTH, D_MODEL)),
        'w_xq': nrm((DEPTH, D_MODEL, X_W), D_MODEL ** -0.5),
        'w_xk': nrm((DEPTH, D_MODEL, X_W), D_MODEL ** -0.5),
        'w_xv': nrm((DEPTH, D_MODEL, X_W), D_MODEL ** -0.5),
        'w_xo': nrm((DEPTH, X_W, D_MODEL), X_W ** -0.5),
        'g_post_x': gain((DEPTH, D_MODEL)),
        'g_pre_ffn': gain((DEPTH, D_MODEL)),
        'w_gate': nrm((DEPTH, D_MODEL, D_FF), D_MODEL ** -0.5),
        'w_up': nrm((DEPTH, D_MODEL, D_FF), D_MODEL ** -0.5),
        'ffn_conv_w': nrm((DEPTH, FFN_CONV, D_FF), FFN_CONV ** -0.5),
        'w_down': nrm((DEPTH, D_FF, D_MODEL), D_FF ** -0.5),
        'g_post_ffn': gain((DEPTH, D_MODEL)),
    }


def reference(x_prompt, x_sample, cache_k, cache_v, cache_idx_k, state_dn_conv, state_dn, state_ssm_conv, state_ssm,
              state_ffn_conv, cache_mem_k, cache_mem_v, page_table, mem_prompt,
              g_pre_mix, w_in, dn_conv_w, dn_a_log, dn_dt_bias, dn_norm_g, ssm_conv_w, ssm_conv_b, ssm_a_log,
              ssm_dt_bias, ssm_d, ssm_norm_g, w_out, g_post_mix, g_pre_x, w_xq, w_xk, w_xv, w_xo, g_post_x,
              g_pre_ffn, w_gate, w_up, ffn_conv_w, w_down, g_post_ffn):
    P = dict(g_pre_mix=g_pre_mix, w_in=w_in, dn_conv_w=dn_conv_w, dn_a_log=dn_a_log, dn_dt_bias=dn_dt_bias,
             dn_norm_g=dn_norm_g, ssm_conv_w=ssm_conv_w, ssm_conv_b=ssm_conv_b, ssm_a_log=ssm_a_log,
             ssm_dt_bias=ssm_dt_bias, ssm_d=ssm_d, ssm_norm_g=ssm_norm_g, w_out=w_out, g_post_mix=g_post_mix,
             g_pre_x=g_pre_x, w_xq=w_xq, w_xo=w_xo, g_post_x=g_post_x, g_pre_ffn=g_pre_ffn,
             w_gate=w_gate, w_up=w_up, ffn_conv_w=ffn_conv_w, w_down=w_down, g_post_ffn=g_post_ffn)
    f32 = jnp.float32
    b = x_prompt.shape[0]
    x = x_prompt
    ps = []
    for l in range(DEPTH):
        mk = (mem_prompt @ w_xk[l]).reshape(b, -1, X_HEADS, X_HEAD_DIM)
        mv = (mem_prompt @ w_xv[l]).reshape(b, -1, X_HEADS, X_HEAD_DIM)
        x, st = hybrid_layer(x, l, P, dsa_prompt, mk, mv,
                             jnp.zeros((b, DN_CONV - 1, 3 * DN_W), x.dtype),
                             jnp.zeros((b, DN_HEADS, DN_HEAD_DIM, DN_HEAD_DIM), f32),
                             jnp.zeros((b, SSM_CONV - 1, SSM_CONV_CH), x.dtype),
                             jnp.zeros((b, SSM_HEADS, SSM_HEAD_DIM, SSM_STATE), f32),
                             jnp.zeros((b, FFN_CONV - 1, D_FF), x.dtype))
        ps.append(st + (mk, mv))
    y_prompt = x
    x = x_sample
    ss = []
    for l in range(DEPTH):
        attn = functools.partial(dsa_sample, k_pool=cache_k[l], v_pool=cache_v[l], ki_pool=cache_idx_k[l],
                                 page_table=page_table)
        x, st = hybrid_layer(x, l, P, attn, cache_mem_k[l], cache_mem_v[l], state_dn_conv[l], state_dn[l],
                             state_ssm_conv[l], state_ssm[l], state_ffn_conv[l])
        ss.append(st)
    y_sample = x

    def stk(states, i):
        return jnp.stack([st[i] for st in states])

    return (y_prompt, y_sample,
            stk(ps, 0), stk(ps, 1), stk(ps, 2), stk(ss, 0), stk(ss, 1), stk(ss, 2),
            stk(ps, 3), stk(ps, 4), stk(ss, 3), stk(ss, 4),
            stk(ps, 5), stk(ps, 6), stk(ss, 5), stk(ss, 6),
            stk(ps, 7), stk(ss, 7),
            stk(ps, 8), stk(ps, 9))
```

```python
import functools

import numpy as np
import jax
import jax.numpy as jnp
from jax import lax
from jax.experimental import pallas as pl
from jax.experimental.pallas import tpu as pltpu

F32, BF16, I32 = jnp.float32, jnp.bfloat16, jnp.int32

D_MODEL = 1024
PAGE_SIZE = 128
A_HEADS, A_HEAD_DIM = 4, 64
IDX_HEADS, IDX_DIM = 8, 32
TOPK_MAX = 256
Q_BLOCK = 128
DN_HEADS, DN_HEAD_DIM, DN_CONV = 4, 64, 4
SSM_HEADS, SSM_HEAD_DIM, SSM_GROUPS, SSM_STATE, SSM_CONV = 8, 64, 2, 128, 4
CHUNK = 64
MEM_LEN = 256
X_HEADS, X_HEAD_DIM = 4, 64
D_FF = 11 * D_MODEL // 4
FFN_CONV = 3
EPS = 1e-6
A_W = A_HEADS * A_HEAD_DIM
DN_W = DN_HEADS * DN_HEAD_DIM
SSM_W = SSM_HEADS * SSM_HEAD_DIM
SSM_CONV_CH = SSM_W + 2 * SSM_GROUPS * SSM_STATE
X_W = X_HEADS * X_HEAD_DIM
IN_SIZES = (A_W, A_W, A_W, IDX_HEADS * IDX_DIM, IDX_DIM, IDX_HEADS,
            3 * DN_W, DN_HEADS, DN_HEADS, DN_W, SSM_W, SSM_CONV_CH, SSM_HEADS)

LANES = 128
SUBLANES = 8
VMEM_LIMIT_BYTES = 56 * 1024 * 1024

SM_KI = 0
SM_WI = SM_KI + IDX_DIM
SM_BETA = SM_WI + IDX_HEADS
SM_ALPHA = SM_BETA + DN_HEADS
SM_DT = SM_ALPHA + DN_HEADS
SM_USED = SM_DT + SSM_HEADS

MASKED = -1e30
BIG = 3.0e38
BISECT_STEPS = 40


def _cparams(*sem):
    return pltpu.CompilerParams(dimension_semantics=sem, vmem_limit_bytes=VMEM_LIMIT_BYTES)


def _rms(x, g):
    return x * lax.rsqrt(jnp.mean(x * x, axis=-1, keepdims=True) + EPS) * g


def _sigmoid(x):
    return 1.0 / (1.0 + jnp.exp(-x))


def _silu(x):
    return x * _sigmoid(x)


def _softplus(x):
    return jnp.maximum(x, 0.0) + jnp.log1p(jnp.exp(-jnp.abs(x)))


def _l2n(x):
    return x * lax.rsqrt(jnp.sum(x * x, axis=-1, keepdims=True) + EPS)


def _dot(a, b):
    return jnp.dot(a.astype(BF16), b.astype(BF16), preferred_element_type=F32)


def _dot_nt(a, b):
    return lax.dot_general(a.astype(BF16), b.astype(BF16), (((1,), (1,)), ((), ())),
                           preferred_element_type=F32)


def _dot_tn(a, b):
    return lax.dot_general(a.astype(BF16), b.astype(BF16), (((0,), (0,)), ((), ())),
                           preferred_element_type=F32)


def _split(x):
    hi = x.astype(BF16)
    return hi, (x - hi.astype(F32)).astype(BF16)


def _dot3(a, b):
    ah, al = _split(a)
    bh, bl = _split(b)
    return (jnp.dot(ah, bh, preferred_element_type=F32) + jnp.dot(ah, bl, preferred_element_type=F32)
            + jnp.dot(al, bh, preferred_element_type=F32))


def _dot_sel(a, r):
    ah, al = _split(a)
    return jnp.dot(ah, r, preferred_element_type=F32) + jnp.dot(al, r, preferred_element_type=F32)


def _seg_cumsum(x, axis, seg):
    pos = lax.broadcasted_iota(I32, x.shape, axis) % seg
    s = 1
    while s < seg:
        x = x + jnp.where(pos >= s, pltpu.roll(x, s, axis), 0.0)
        s *= 2
    return x


def _proj_kernel(*refs, n_w, has_norm, has_t):
    x_ref = refs[0]
    pos = 1
    g_ref = None
    if has_norm:
        g_ref = refs[pos]
        pos += 1
    w_refs = refs[pos:pos + n_w]
    pos += n_w
    wt_ref = None
    if has_t:
        wt_ref = refs[pos]
        pos += 1
    o_refs = refs[pos:pos + n_w]
    pos += n_w
    x = x_ref[...]
    if has_norm:
        x = _rms(x, g_ref[...])
    xb = x.astype(BF16)
    for w_ref, o_ref in zip(w_refs, o_refs):
        o_ref[...] = jnp.dot(xb, w_ref[...], preferred_element_type=F32).astype(o_ref.dtype)
    if has_t:
        refs[pos][...] = lax.dot_general(wt_ref[...], xb, (((1,), (1,)), ((), ())),
                                         preferred_element_type=F32)


def _proj(x, g, ws, wt=None, tm=256):
    t, d = x.shape
    tm = min(tm, t)
    args, in_specs = [x], [pl.BlockSpec((tm, d), lambda i: (i, 0))]
    if g is not None:
        args.append(g.reshape(1, d))
        in_specs.append(pl.BlockSpec((1, d), lambda i: (0, 0)))
    for w in ws:
        args.append(w)
        in_specs.append(pl.BlockSpec(w.shape, lambda i: (0, 0)))
    out_shape = [jax.ShapeDtypeStruct((t, w.shape[1]), F32) for w in ws]
    out_specs = [pl.BlockSpec((tm, w.shape[1]), lambda i: (i, 0)) for w in ws]
    if wt is not None:
        args.append(wt)
        in_specs.append(pl.BlockSpec(wt.shape, lambda i: (0, 0)))
        out_shape.append(jax.ShapeDtypeStruct((wt.shape[0], t), F32))
        out_specs.append(pl.BlockSpec((wt.shape[0], tm), lambda i: (0, i)))
    return pl.pallas_call(
        functools.partial(_proj_kernel, n_w=len(ws), has_norm=g is not None, has_t=wt is not None),
        grid=(t // tm,), in_specs=in_specs, out_specs=out_specs, out_shape=out_shape,
        compiler_params=_cparams("parallel"), name="proj")(*args)


def _mm_post_kernel(*refs, n_a):
    x_ref, g_ref = refs[0], refs[1]
    a_refs, w_refs, o_ref = refs[2:2 + n_a], refs[2 + n_a:2 + 2 * n_a], refs[2 + 2 * n_a]
    acc = None
    for a_ref, w_ref in zip(a_refs, w_refs):
        part = jnp.dot(a_ref[...].astype(BF16), w_ref[...], preferred_element_type=F32)
        acc = part if acc is None else acc + part
    o_ref[...] = x_ref[...] + _rms(acc, g_ref[...])


def _mm_post(x, g, a_list, w_list, tm=256):
    t, d = x.shape
    tm = min(tm, t)
    args = [x, g.reshape(1, d)] + list(a_list) + list(w_list)
    in_specs = [pl.BlockSpec((tm, d), lambda i: (i, 0)), pl.BlockSpec((1, d), lambda i: (0, 0))]
    in_specs += [pl.BlockSpec((tm, a.shape[1]), lambda i: (i, 0)) for a in a_list]
    in_specs += [pl.BlockSpec(w.shape, lambda i: (0, 0)) for w in w_list]
    return pl.pallas_call(
        functools.partial(_mm_post_kernel, n_a=len(a_list)),
        grid=(t // tm,), in_specs=in_specs, out_specs=pl.BlockSpec((tm, d), lambda i: (i, 0)),
        out_shape=jax.ShapeDtypeStruct((t, d), F32),
        compiler_params=_cparams("parallel"), name="mm_post")(*args)


def _index_score(d, w):
    s = None
    for h in range(IDX_HEADS):
        part = jnp.maximum(d[:, h * LANES:(h + 1) * LANES], 0.0) * w[:, h * LANES:(h + 1) * LANES]
        s = part if s is None else s + part
    return s


def _fold_rows(x, op):
    parts = [x[r:r + SUBLANES, :] for r in range(0, x.shape[0], SUBLANES)]
    while len(parts) > 1:
        nxt = [op(parts[a], parts[a + 1]) for a in range(0, len(parts) - 1, 2)]
        parts = nxt + ([parts[-1]] if len(parts) % 2 else [])
    return parts[0]


def _loop_pairs(n, body, init, group=2):
    c, start = init, 0
    while group >= 1:
        def grouped(p, c, start=start, group=group):
            for u in range(group):
                c = body(start + group * p + u, c)
            return c
        trips = (n - start) // group
        c = lax.fori_loop(0, trips, grouped, c)
        start = start + group * trips
        group //= 2
    return c


def _count_ge(sc_ref, n_tiles, thr):
    def body(j, acc):
        return acc + _fold_rows(jnp.where(sc_ref[j] >= thr, 1.0, 0.0), jnp.add)
    acc = _loop_pairs(n_tiles, body, jnp.zeros((SUBLANES, LANES), F32), group=4)
    return jnp.sum(acc, axis=0, keepdims=True)


def _topk_bias(sc_ref, bias_ref, lower_ref, n_tiles, k_sel):
    kf = float(k_sel)
    inf = jnp.inf
    vec8 = lambda v: jnp.full((SUBLANES, LANES), v, F32)

    def stats(j, c):
        mn, mx, nv = c
        s = sc_ref[j]
        valid = s > -inf
        return (jnp.minimum(mn, _fold_rows(jnp.where(valid, s, inf), jnp.minimum)),
                jnp.maximum(mx, _fold_rows(s, jnp.maximum)),
                nv + _fold_rows(jnp.where(valid, 1.0, 0.0), jnp.add))

    mn, mx, nv = lax.fori_loop(0, n_tiles, stats, (vec8(inf), vec8(-inf), vec8(0.0)))
    s_min = jnp.min(mn, axis=0, keepdims=True)
    s_max = jnp.max(mx, axis=0, keepdims=True)
    n_valid = jnp.sum(nv, axis=0, keepdims=True)
    c_max = _count_ge(sc_ref, n_tiles, s_max)
    all_taken = n_valid <= kf
    top_ties = jnp.logical_and(jnp.logical_not(all_taken), c_max >= kf)
    searching = jnp.logical_not(jnp.logical_or(all_taken, top_ties))

    def a_cond(c):
        it, lo, hi, c_lo, c_hi = c
        open_ = jnp.logical_and(searching, c_lo != kf)
        return jnp.logical_and(it < BISECT_STEPS, jnp.max(jnp.where(open_, 1.0, 0.0)) > 0.0)

    def halve(lo, hi, c_lo, c_hi):
        mid = 0.5 * lo + 0.5 * hi
        cm = _count_ge(sc_ref, n_tiles, mid)
        up = jnp.logical_and(searching, cm >= kf)
        dn = jnp.logical_and(searching, cm < kf)
        return jnp.where(up, mid, lo), jnp.where(dn, mid, hi), jnp.where(up, cm, c_lo), jnp.where(dn, cm, c_hi)

    def a_body(c):
        return (c[0] + 4,) + halve(*halve(*halve(*halve(*c[1:]))))

    _, lo, hi, c_lo, c_hi = lax.while_loop(
        a_cond, a_body, (jnp.int32(0), jnp.where(all_taken, -BIG, s_min), s_max, n_valid, c_max))

    settled = jnp.logical_or(all_taken, jnp.logical_and(searching, c_lo == kf))
    thr0 = jnp.where(top_ties, s_max, lo)
    need0 = jnp.where(top_ties, kf, BIG)
    pending0 = jnp.where(jnp.logical_or(settled, top_ties), 0.0, 1.0)

    def b_cond(c):
        return jnp.max(c[4]) > 0.0

    def b_body(c):
        hi, c_hi, thr, need, pending = c

        def below(j, m):
            s = sc_ref[j]
            return jnp.maximum(m, _fold_rows(jnp.where(s < hi, s, -inf), jnp.maximum))

        nxt = jnp.max(lax.fori_loop(0, n_tiles, below, vec8(-inf)), axis=0, keepdims=True)
        cn = _count_ge(sc_ref, n_tiles, nxt)
        hit = jnp.logical_and(pending > 0.0, cn >= kf)
        still = jnp.logical_and(pending > 0.0, cn < kf)
        return (jnp.where(still, nxt, hi), jnp.where(still, cn, c_hi), jnp.where(hit, nxt, thr),
                jnp.where(hit, kf - c_hi, need), jnp.where(still, 1.0, 0.0))

    _, _, thr, need, _ = lax.while_loop(b_cond, b_body, (hi, c_hi, thr0, need0, pending0))

    def fill_ordered(j, seen):
        s = sc_ref[j]
        tie = s == thr
        tie_f = jnp.where(tie, 1.0, 0.0)
        before = jnp.dot(lower_ref[...], tie_f.astype(BF16), preferred_element_type=F32) + seen
        take = jnp.logical_or(s > thr, jnp.logical_and(tie, before < need))
        bias_ref[j] = jnp.where(take, 0.0, MASKED)
        return seen + jnp.sum(tie_f, axis=0, keepdims=True)

    def fill_all(j, c):
        bias_ref[j] = jnp.where(sc_ref[j] >= thr, 0.0, MASKED)
        return c

    def ordered():
        lax.fori_loop(0, n_tiles, fill_ordered, jnp.zeros((1, LANES), F32))

    def plain():
        lax.fori_loop(0, n_tiles, fill_all, 0)

    lax.cond(jnp.max(jnp.where(need < BIG, 1.0, 0.0)) > 0.0, ordered, plain)


def _dsa_prompt_kernel(q_ref, qi_ref, smq_ref, kall_ref, vall_ref, small_ref, lower_ref, o_ref,
                       ki_scr, k_scr, vt_scr, sc_scr, bias_scr, lg_scr, *, k_sel, nt):
    i = pl.program_id(1)

    @pl.when(i == 0)
    def _():
        def prep(j, c):
            rows = pl.ds(pl.multiple_of(j * Q_BLOCK, Q_BLOCK), Q_BLOCK)
            kt = kall_ref[0, rows, :]
            for h in range(A_HEADS):
                k_scr[j, h] = kt[:, h * A_HEAD_DIM:(h + 1) * A_HEAD_DIM].astype(BF16)
            vt_scr[j] = vall_ref[0, rows, :].T.astype(BF16)
            ki_scr[j] = small_ref[0, rows, :][:, SM_KI:SM_KI + IDX_DIM].astype(BF16)
            return c
        lax.fori_loop(0, nt, prep, 0)

    qi_t = qi_ref[0].T
    qit = jnp.concatenate([qi_t[h * IDX_DIM:(h + 1) * IDX_DIM, :] for h in range(IDX_HEADS)],
                          axis=1).astype(BF16)
    sm_t = smq_ref[0].T
    w = jnp.concatenate([sm_t[SM_WI + h:SM_WI + h + 1, :] for h in range(IDX_HEADS)], axis=1)
    w = w * (IDX_HEADS ** -0.5 * IDX_DIM ** -0.5)

    def tile_scores(j):
        d = jnp.dot(ki_scr[j], qit, preferred_element_type=F32)
        return _index_score(d, w)

    def score_body(j, c):
        sc_scr[j] = tile_scores(j)
        return c

    _loop_pairs(i, score_body, 0, group=4)
    key_pos = lax.broadcasted_iota(I32, (Q_BLOCK, Q_BLOCK), 0)
    qry_pos = lax.broadcasted_iota(I32, (Q_BLOCK, Q_BLOCK), 1)
    sc_scr[i] = jnp.where(key_pos <= qry_pos, tile_scores(i), -jnp.inf)

    _topk_bias(sc_scr, bias_scr, lower_ref, i + 1, k_sel)

    heads = range(A_HEADS)
    rows = [slice(h * A_HEAD_DIM, (h + 1) * A_HEAD_DIM) for h in heads]
    q_t = q_ref[0].T * A_HEAD_DIM ** -0.5
    qh = [q_t[rows[h], :].astype(BF16) for h in heads]

    def logits_body(j, ms):
        b = bias_scr[j]
        out = []
        for h in heads:
            s = jnp.dot(k_scr[j, h], qh[h], preferred_element_type=F32) + b
            lg_scr[j, h] = s
            out.append(jnp.maximum(ms[h], _fold_rows(s, jnp.maximum)))
        return tuple(out)

    ms = _loop_pairs(i + 1, logits_body, tuple(jnp.full((SUBLANES, LANES), MASKED, F32) for _ in heads), group=4)
    m = [jnp.max(x, axis=0, keepdims=True) for x in ms]

    def value_body(j, c):
        ls, accs = c
        ls_out, accs_out = [], []
        for h in heads:
            p = jnp.exp(lg_scr[j, h] - m[h])
            ls_out.append(ls[h] + _fold_rows(p, jnp.add))
            accs_out.append(accs[h] + jnp.dot(vt_scr[j, rows[h], :], p.astype(BF16),
                                              preferred_element_type=F32))
        return tuple(ls_out), tuple(accs_out)

    init = (tuple(jnp.zeros((SUBLANES, LANES), F32) for _ in heads),
            tuple(jnp.zeros((A_HEAD_DIM, Q_BLOCK), F32) for _ in heads))
    ls, accs = _loop_pairs(i + 1, value_body, init, group=4)
    o_t = jnp.concatenate([accs[h] / jnp.sum(ls[h], axis=0, keepdims=True) for h in heads], axis=0)
    o_ref[0] = o_t.T


def _lower_ones():
    r = lax.broadcasted_iota(I32, (LANES, LANES), 0)
    c = lax.broadcasted_iota(I32, (LANES, LANES), 1)
    return (c < r).astype(BF16)


def _dsa_prompt(a4, small):
    b, s, _ = a4.shape
    nt = s // Q_BLOCK
    k_sel = min(TOPK_MAX, s // 4)
    blk = lambda c: pl.BlockSpec((1, Q_BLOCK, A_W), lambda bi, i: (bi, i, c))
    full = lambda c: pl.BlockSpec((1, s, A_W), lambda bi, i: (bi, 0, c))
    return pl.pallas_call(
        functools.partial(_dsa_prompt_kernel, k_sel=k_sel, nt=nt),
        grid=(b, nt),
        in_specs=[blk(0), blk(3), pl.BlockSpec((1, Q_BLOCK, LANES), lambda bi, i: (bi, i, 0)),
                  full(1), full(2), pl.BlockSpec((1, s, LANES), lambda bi, i: (bi, 0, 0)),
                  pl.BlockSpec((LANES, LANES), lambda bi, i: (0, 0))],
        out_specs=pl.BlockSpec((1, Q_BLOCK, A_W), lambda bi, i: (bi, i, 0)),
        out_shape=jax.ShapeDtypeStruct((b, s, A_W), F32),
        scratch_shapes=[pltpu.VMEM((nt, Q_BLOCK, IDX_DIM), BF16),
                        pltpu.VMEM((nt, A_HEADS, Q_BLOCK, A_HEAD_DIM), BF16),
                        pltpu.VMEM((nt, A_W, Q_BLOCK), BF16),
                        pltpu.VMEM((nt, Q_BLOCK, Q_BLOCK), F32), pltpu.VMEM((nt, Q_BLOCK, Q_BLOCK), F32),
                        pltpu.VMEM((nt, A_HEADS, Q_BLOCK, Q_BLOCK), F32)],
        compiler_params=_cparams("parallel", "arbitrary"), name="dsa_prompt",
    )(a4, a4, small, a4, a4, small, _lower_ones())


def _dsa_sample_score_kernel(pt_ref, qi_ref, w_ref, kinew_ref, *rest, n_pages):
    page_refs, out_ref = rest[:n_pages], rest[n_pages]
    qi = qi_ref[0]
    w = w_ref[0] * (IDX_HEADS ** -0.5 * IDX_DIM ** -0.5)
    qb = qi.astype(BF16)
    for p in range(n_pages):
        d = jnp.dot(qb, page_refs[p][0, 0].astype(BF16), preferred_element_type=F32)
        out_ref[0, :, p * PAGE_SIZE:(p + 1) * PAGE_SIZE] = jnp.sum(jnp.maximum(d, 0.0) * w, axis=0, keepdims=True)
    d_new = jnp.sum(qb.astype(F32) * kinew_ref[0].astype(BF16).astype(F32), axis=1, keepdims=True)
    s_new = jnp.sum(jnp.maximum(d_new, 0.0) * w, axis=0, keepdims=True)
    out_ref[0, :, n_pages * PAGE_SIZE:] = jnp.broadcast_to(s_new, (1, PAGE_SIZE))


def _dsa_sample_select_kernel(sc_ref, lower_ref, bias_ref, sc_scr, bias_scr, *, n_tiles, last_valid, k_sel):
    key_pos = lax.broadcasted_iota(I32, (LANES, LANES), 0)
    for j in range(n_tiles):
        s = sc_ref[:, j * PAGE_SIZE:(j + 1) * PAGE_SIZE].T
        if j == n_tiles - 1:
            s = jnp.where(key_pos < last_valid, s, -jnp.inf)
        sc_scr[j] = s
    _topk_bias(sc_scr, bias_scr, lower_ref, n_tiles, k_sel)
    for j in range(n_tiles):
        bias_ref[:, j * PAGE_SIZE:(j + 1) * PAGE_SIZE] = bias_scr[j].T


def _lane_column(ref, b):
    mine = lax.broadcasted_iota(I32, ref.shape, 1) == b
    return jnp.sum(jnp.where(mine, ref[...], 0.0), axis=1, keepdims=True)


def _dsa_sample_attend_kernel(pt_ref, qt_ref, knt_ref, vnt_ref, bias_ref, *rest, n_pages):
    k_pages, v_pages = rest[:n_pages], rest[n_pages:2 * n_pages]
    ot_ref, lg_scr = rest[2 * n_pages], rest[2 * n_pages + 1]
    b = pl.program_id(0)

    @pl.when(b == 0)
    def _():
        ot_ref[...] = jnp.zeros(ot_ref.shape, F32)

    q = _lane_column(qt_ref, b) * A_HEAD_DIM ** -0.5
    k_new = _lane_column(knt_ref, b)
    v_new = _lane_column(vnt_ref, b)
    heads = range(A_HEADS)
    rows = [slice(h * A_HEAD_DIM, (h + 1) * A_HEAD_DIM) for h in heads]
    for h in heads:
        for p in range(n_pages):
            kt = k_pages[p][0, 0, h]
            lg_scr[h, p:p + 1, :] = (jnp.sum(kt * q[rows[h]], axis=0, keepdims=True)
                                     + bias_ref[0, :, p * PAGE_SIZE:(p + 1) * PAGE_SIZE])
        lg_new = jnp.sum(k_new[rows[h]] * q[rows[h]], axis=0, keepdims=True)
        lg_scr[h, n_pages:n_pages + 1, :] = lg_new + bias_ref[0, :, n_pages * PAGE_SIZE:]
    out = []
    for h in heads:
        lg = lg_scr[h, 0:n_pages + 1, :]
        m = jnp.max(jnp.max(lg, axis=1, keepdims=True), axis=0, keepdims=True)
        e = jnp.exp(lg - m)
        pn = e / jnp.sum(jnp.sum(e, axis=1, keepdims=True), axis=0, keepdims=True)
        acc = jnp.zeros((A_HEAD_DIM, PAGE_SIZE), F32)
        for p in range(n_pages):
            acc = acc + v_pages[p][0, 0, h] * pn[p:p + 1, :]
        out.append(jnp.sum(acc, axis=1, keepdims=True) + v_new[rows[h]] * pn[n_pages:n_pages + 1, 0:1])
    col = jnp.concatenate(out, axis=0)
    mine = lax.broadcasted_iota(I32, ot_ref.shape, 1) == b
    ot_ref[...] = jnp.where(mine, col, ot_ref[...])


def _dsa_sample(l, q, k_new, v_new, qi, ki_new, wi, cache_k, cache_v, cache_idx_k, page_table):
    b = q.shape[0]
    assert b == LANES
    n_pages = page_table.shape[1]
    n_tiles = n_pages + 1
    width = n_tiles * PAGE_SIZE
    k_sel = min(TOPK_MAX, (n_pages * PAGE_SIZE + 1) // 4)
    ckt = cache_k.transpose(0, 1, 3, 4, 2)
    cvt = cache_v.transpose(0, 1, 3, 4, 2)
    cit = cache_idx_k.transpose(0, 1, 3, 2)

    scores = pl.pallas_call(
        functools.partial(_dsa_sample_score_kernel, n_pages=n_pages),
        grid_spec=pltpu.PrefetchScalarGridSpec(
            num_scalar_prefetch=1, grid=(b,),
            in_specs=[pl.BlockSpec((1, IDX_HEADS, IDX_DIM), lambda bi, pt: (bi, 0, 0)),
                      pl.BlockSpec((1, IDX_HEADS, 1), lambda bi, pt: (bi, 0, 0)),
                      pl.BlockSpec((1, 1, IDX_DIM), lambda bi, pt: (bi, 0, 0))]
            + [pl.BlockSpec((1, 1, IDX_DIM, PAGE_SIZE), lambda bi, pt, p=p: (l, pt[bi, p], 0, 0))
               for p in range(n_pages)],
            out_specs=pl.BlockSpec((1, 1, width), lambda bi, pt: (bi, 0, 0))),
        out_shape=jax.ShapeDtypeStruct((b, 1, width), F32),
        compiler_params=_cparams("parallel"), name="dsa_sample_score",
    )(page_table, qi.reshape(b, IDX_HEADS, IDX_DIM), wi.reshape(b, IDX_HEADS, 1), ki_new.reshape(b, 1, IDX_DIM),
      *([cit] * n_pages))

    bias = pl.pallas_call(
        functools.partial(_dsa_sample_select_kernel, n_tiles=n_tiles, last_valid=1, k_sel=k_sel),
        grid=(1,),
        in_specs=[pl.BlockSpec((b, width), lambda i: (0, 0)), pl.BlockSpec((LANES, LANES), lambda i: (0, 0))],
        out_specs=pl.BlockSpec((b, width), lambda i: (0, 0)),
        out_shape=jax.ShapeDtypeStruct((b, width), F32),
        scratch_shapes=[pltpu.VMEM((n_tiles, PAGE_SIZE, LANES), F32), pltpu.VMEM((n_tiles, PAGE_SIZE, LANES), F32)],
        compiler_params=_cparams("arbitrary"), name="dsa_sample_select",
    )(scores.reshape(b, width), _lower_ones())

    resident = pl.BlockSpec((A_W, LANES), lambda bi, pt: (0, 0))
    page = lambda p: pl.BlockSpec((1, 1, A_HEADS, A_HEAD_DIM, PAGE_SIZE), lambda bi, pt, p=p: (l, pt[bi, p], 0, 0, 0))
    ot = pl.pallas_call(
        functools.partial(_dsa_sample_attend_kernel, n_pages=n_pages),
        grid_spec=pltpu.PrefetchScalarGridSpec(
            num_scalar_prefetch=1, grid=(b,),
            in_specs=[resident, resident, resident, pl.BlockSpec((1, 1, width), lambda bi, pt: (bi, 0, 0))]
            + [page(p) for p in range(n_pages)] * 2,
            out_specs=resident,
            scratch_shapes=[pltpu.VMEM((A_HEADS, 3 * SUBLANES, PAGE_SIZE), F32)]),
        out_shape=jax.ShapeDtypeStruct((A_W, b), F32),
        compiler_params=_cparams("arbitrary"), name="dsa_sample_attend",
    )(page_table, q.T, k_new.T, v_new.T, bias.reshape(b, 1, width), *([ckt] * n_pages), *([cvt] * n_pages))
    return ot.T


def _conv_tile(cbuf, x, cw, width, tm):
    cbuf[SUBLANES:SUBLANES + tm, :] = x
    y = x * cw[width - 1:width, :]
    for j in range(width - 1):
        off = SUBLANES - (width - 1) + j
        y = y + cbuf[off:off + tm, :] * cw[j:j + 1, :]
    cbuf[SUBLANES - (width - 1):SUBLANES, :] = cbuf[SUBLANES + tm - (width - 1):SUBLANES + tm, :]
    return y


def _nilpotent_inverses(mats, order):
    shape = mats[0].shape
    eye = (lax.broadcasted_iota(I32, shape, 0) == lax.broadcasted_iota(I32, shape, 1)).astype(F32)
    xs = [eye - a for a in mats]
    ps = [_dot3(a, a) for a in mats]
    n = 2
    while True:
        xs = [x + _dot3(x, p) for x, p in zip(xs, ps)]
        n *= 2
        if n >= order:
            return xs
        ps = [_dot3(p, p) for p in ps]


def _dn_prompt_kernel(dqkv_ref, gate_ref, sm_ref, smt_ref, buf0_ref, s0_ref, cw_ref, alog_r_ref, dtb_r_ref,
                      alog_c_ref, dtb_c_ref, ng_ref, o_ref, sfin_ref, cbuf, s_scr, *, tm):
    t = pl.program_id(1)

    @pl.when(t == 0)
    def _():
        cbuf[SUBLANES - (DN_CONV - 1):SUBLANES, :] = buf0_ref[0]
        s_scr[...] = s0_ref[0]

    y = _silu(_conv_tile(cbuf, dqkv_ref[0], cw_ref[...], DN_CONV, tm))
    sm = sm_ref[0]
    smt = smt_ref[SM_BETA:SM_BETA + SUBLANES, :]
    beta_c = _sigmoid(sm)
    g_c = -jnp.exp(alog_r_ref[...]) * _softplus(sm + dtb_r_ref[...])
    g_r = -jnp.exp(alog_c_ref[...]) * _softplus(smt + dtb_c_ref[...])
    gc_c = _seg_cumsum(g_c, 0, CHUNK)
    gc_r = _seg_cumsum(g_r, 1, CHUNK)
    gate = gate_ref[0]
    hd, nh, n = DN_HEAD_DIM, DN_HEADS, DN_HEADS * CHUNK
    ri = lax.broadcasted_iota(I32, (n, n), 0)
    ci = lax.broadcasted_iota(I32, (n, n), 1)
    same = (ri // CHUNK) == (ci // CHUNK)
    tril = jnp.logical_and(same, ri >= ci)
    strict = jnp.logical_and(same, ri > ci)
    stack = lambda a: jnp.concatenate([a[:, h * hd:(h + 1) * hd] for h in range(nh)], axis=0)
    pre = []
    for c in range(tm // CHUNK):
        r = slice(c * CHUNK, (c + 1) * CHUNK)
        last = slice((c + 1) * CHUNK - 1, (c + 1) * CHUNK)
        col = lambda a, off, rows=r: jnp.concatenate([a[rows, off + h:off + h + 1] for h in range(nh)], axis=0)
        q = _l2n(stack(y[r, 0:DN_W])) * hd ** -0.5
        k = _l2n(stack(y[r, DN_W:2 * DN_W]))
        v = stack(y[r, 2 * DN_W:3 * DN_W])
        beta = col(beta_c, SM_BETA)
        gcol = col(gc_c, SM_ALPHA)
        grow = jnp.concatenate([gc_r[nh + h:nh + h + 1, r] for h in range(nh)], axis=1)
        g_last = [gc_c[last, SM_ALPHA + h:SM_ALPHA + h + 1] for h in range(nh)]
        g_last_col = jnp.concatenate([jnp.broadcast_to(g, (CHUNK, 1)) for g in g_last], axis=0)
        decay = jnp.where(tril, jnp.exp(jnp.where(tril, gcol - grow, 0.0)), 0.0)
        kb = k * beta
        pre.append(dict(a=jnp.where(strict, _dot_nt(kb, k) * decay, 0.0),
                        rhs=jnp.concatenate([v * beta, kb * jnp.exp(gcol)], axis=1),
                        attn=_dot_nt(q, k) * decay, k_dec=k * jnp.exp(g_last_col - gcol),
                        qe=q * jnp.exp(gcol), g_last=g_last))
    t_invs = _nilpotent_inverses([p["a"] for p in pre], CHUNK)
    uws = [_dot(t_inv, p["rhs"]) for t_inv, p in zip(t_invs, pre)]
    for c in range(tm // CHUNK):
        r = slice(c * CHUNK, (c + 1) * CHUNK)
        u, w = uws[c][:, :hd], uws[c][:, hd:]
        attn, k_dec, qe, g_last = pre[c]["attn"], pre[c]["k_dec"], pre[c]["qe"], pre[c]["g_last"]
        v_new, o_state = [], []
        for h in range(nh):
            hs = slice(h * CHUNK, (h + 1) * CHUNK)
            s = s_scr[h]
            ws = _dot(jnp.concatenate([w[hs], qe[hs]], axis=0), s)
            vn = u[hs] - ws[:CHUNK]
            v_new.append(vn)
            o_state.append(ws[CHUNK:])
            s_scr[h] = s * jnp.exp(g_last[h]) + _dot_tn(k_dec[hs], vn)
        o = jnp.concatenate(o_state, axis=0) + _dot(attn, jnp.concatenate(v_new, axis=0))
        o = _rms(o, ng_ref[...])
        for h in range(nh):
            o_ref[0, r, h * hd:(h + 1) * hd] = o[h * CHUNK:(h + 1) * CHUNK] * _silu(gate[r, h * hd:(h + 1) * hd])
    sfin_ref[0] = s_scr[...]


def _dn_prompt(dqkv, gate, small, small_t, buf0, s0, prm, tm=256):
    b, s, _ = dqkv.shape
    tm = min(tm, s)
    nt = s // tm
    vec = lambda shape: pl.BlockSpec(shape, lambda bi, t: (0,) * len(shape))
    return pl.pallas_call(
        functools.partial(_dn_prompt_kernel, tm=tm),
        grid=(b, nt),
        in_specs=[pl.BlockSpec((1, tm, 3 * DN_W), lambda bi, t: (bi, t, 0)),
                  pl.BlockSpec((1, tm, DN_W), lambda bi, t: (bi, t, 0)),
                  pl.BlockSpec((1, tm, LANES), lambda bi, t: (bi, t, 0)),
                  pl.BlockSpec((LANES, tm), lambda bi, t: (0, bi * nt + t)),
                  pl.BlockSpec((1, DN_CONV - 1, 3 * DN_W), lambda bi, t: (bi, 0, 0)),
                  pl.BlockSpec((1, DN_HEADS, DN_HEAD_DIM, DN_HEAD_DIM), lambda bi, t: (bi, 0, 0, 0)),
                  vec((DN_CONV, 3 * DN_W)), vec((1, LANES)), vec((1, LANES)),
                  vec((SUBLANES, 1)), vec((SUBLANES, 1)), vec((1, DN_HEAD_DIM))],
        out_specs=[pl.BlockSpec((1, tm, DN_W), lambda bi, t: (bi, t, 0)),
                   pl.BlockSpec((1, DN_HEADS, DN_HEAD_DIM, DN_HEAD_DIM), lambda bi, t: (bi, 0, 0, 0))],
        out_shape=[jax.ShapeDtypeStruct((b, s, DN_W), F32),
                   jax.ShapeDtypeStruct((b, DN_HEADS, DN_HEAD_DIM, DN_HEAD_DIM), F32)],
        scratch_shapes=[pltpu.VMEM((SUBLANES + tm, 3 * DN_W), F32),
                        pltpu.VMEM((DN_HEADS, DN_HEAD_DIM, DN_HEAD_DIM), F32)],
        compiler_params=_cparams("parallel", "arbitrary"), name="dn_prompt",
    )(dqkv, gate, small, small_t, buf0, s0, prm["dn_cw"], prm["dn_alog_r"], prm["dn_dtb_r"],
      prm["dn_alog_c"], prm["dn_dtb_c"], prm["dn_ng"])


def _dn_sample_kernel(x_ref, b0_ref, b1_ref, b2_ref, sm_ref, gate_ref, s_ref, cw_ref, alog_ref, dtb_ref, ng_ref,
                      rk_ref, rv_ref, rvt_ref, o_ref, snew_ref):
    cw = cw_ref[...]
    y = _silu(b0_ref[...] * cw[0:1] + b1_ref[...] * cw[1:2] + b2_ref[...] * cw[2:3] + x_ref[...] * cw[3:4])
    sm = sm_ref[...]
    beta_all = _sigmoid(sm)
    eg_all = jnp.exp(-jnp.exp(alog_ref[...]) * _softplus(sm + dtb_ref[...]))
    gate = gate_ref[...]
    sz = DN_HEAD_DIM * DN_HEAD_DIM
    for h in range(DN_HEADS):
        hs = slice(h * DN_HEAD_DIM, (h + 1) * DN_HEAD_DIM)
        q = _l2n(y[:, h * DN_HEAD_DIM:(h + 1) * DN_HEAD_DIM]) * DN_HEAD_DIM ** -0.5
        k = _l2n(y[:, DN_W + h * DN_HEAD_DIM:DN_W + (h + 1) * DN_HEAD_DIM])
        v = y[:, 2 * DN_W + h * DN_HEAD_DIM:2 * DN_W + (h + 1) * DN_HEAD_DIM]
        beta = beta_all[:, SM_BETA + h:SM_BETA + h + 1]
        eg = eg_all[:, SM_ALPHA + h:SM_ALPHA + h + 1]
        s = s_ref[:, h * sz:(h + 1) * sz]
        kx = _dot_sel(k, rk_ref[...])
        ks = _dot_sel(kx * s, rvt_ref[...])
        v_new = beta * (v - eg * ks)
        qs = _dot_sel(_dot_sel(q, rk_ref[...]) * s, rvt_ref[...])
        o = eg * qs + jnp.sum(q * k, axis=-1, keepdims=True) * v_new
        snew_ref[:, h * sz:(h + 1) * sz] = s * eg + kx * _dot_sel(v_new, rv_ref[...])
        o_ref[:, hs] = _rms(o, ng_ref[...]) * _silu(gate[:, hs])


def _expand_mats(outer, inner):
    col = lax.broadcasted_iota(I32, (outer, outer * inner), 1)
    row = lax.broadcasted_iota(I32, (outer, outer * inner), 0)
    return (col // inner == row).astype(BF16), (col % inner == row).astype(BF16)


def _dn_sample(dqkv, buf, small, gate, state, prm, tb=32):
    b = dqkv.shape[0]
    sz = DN_HEADS * DN_HEAD_DIM * DN_HEAD_DIM
    rk, rv = _expand_mats(DN_HEAD_DIM, DN_HEAD_DIM)
    row = lambda w: pl.BlockSpec((tb, w), lambda i: (i, 0))
    vec = lambda shape: pl.BlockSpec(shape, lambda i: (0,) * len(shape))
    o, s_new = pl.pallas_call(
        _dn_sample_kernel, grid=(b // tb,),
        in_specs=[row(3 * DN_W)] * 4 + [row(LANES), row(DN_W), row(sz),
                  vec((DN_CONV, 3 * DN_W)), vec((1, LANES)), vec((1, LANES)), vec((1, DN_HEAD_DIM)),
                  vec(rk.shape), vec(rv.shape), vec(rv.T.shape)],
        out_specs=[row(DN_W), row(sz)],
        out_shape=[jax.ShapeDtypeStruct((b, DN_W), F32), jax.ShapeDtypeStruct((b, sz), F32)],
        compiler_params=_cparams("parallel"), name="dn_sample",
    )(dqkv, buf[:, 0], buf[:, 1], buf[:, 2], small, gate, state.reshape(b, sz),
      prm["dn_cw"], prm["dn_alog_r"], prm["dn_dtb_r"], prm["dn_ng"], rk, rv, rv.T)
    return o, s_new.reshape(state.shape)


def _ssd_prompt_kernel(x_ref, z_ref, sm_ref, smt_ref, buf0_ref, h0_ref, cw_ref, cb_ref, alog_r_ref, dtb_r_ref,
                       alog_c_ref, dtb_c_ref, d_ref, ng_ref, o_ref, hfin_ref, cbuf, h_scr, y_scr, *, tm):
    t = pl.program_id(1)

    @pl.when(t == 0)
    def _():
        cbuf[SUBLANES - (SSM_CONV - 1):SUBLANES, :] = buf0_ref[0]
        h_scr[...] = h0_ref[0]

    xbc = _silu(_conv_tile(cbuf, x_ref[0], cw_ref[...], SSM_CONV, tm) + cb_ref[...])
    gn = SSM_GROUPS * SSM_STATE
    dt_c = _softplus(sm_ref[0] + dtb_r_ref[...])
    dt_r = _softplus(smt_ref[SM_DT:SM_DT + SSM_HEADS, :] + dtb_c_ref[...])
    ac_c = _seg_cumsum(dt_c * -jnp.exp(alog_r_ref[...]), 0, CHUNK)
    ac_r = _seg_cumsum(dt_r * -jnp.exp(alog_c_ref[...]), 1, CHUNK)
    per_group, hd = SSM_HEADS // SSM_GROUPS, SSM_HEAD_DIM
    n = per_group * CHUNK
    ri = lax.broadcasted_iota(I32, (n, n), 0)
    ci = lax.broadcasted_iota(I32, (n, n), 1)
    tril = jnp.logical_and((ri // CHUNK) == (ci // CHUNK), ri >= ci)
    pre = {}
    for c in range(tm // CHUNK):
        r = slice(c * CHUNK, (c + 1) * CHUNK)
        last = slice((c + 1) * CHUNK - 1, (c + 1) * CHUNK)
        for g in range(SSM_GROUPS):
            heads = range(g * per_group, (g + 1) * per_group)
            bg = xbc[r, SSM_W + g * SSM_STATE:SSM_W + (g + 1) * SSM_STATE]
            cg = xbc[r, SSM_W + gn + g * SSM_STATE:SSM_W + gn + (g + 1) * SSM_STATE]
            xg = xbc[r, g * per_group * hd:(g + 1) * per_group * hd]
            acol = jnp.concatenate([ac_c[r, SM_DT + h:SM_DT + h + 1] for h in heads], axis=0)
            arow = jnp.concatenate([ac_r[h:h + 1, r] for h in heads], axis=1)
            dt_row = jnp.concatenate([dt_r[h:h + 1, r] for h in heads], axis=1)
            a_last = [ac_c[last, SM_DT + h:SM_DT + h + 1] for h in heads]
            lmat = jnp.where(tril, jnp.exp(jnp.where(tril, acol - arow, 0.0)), 0.0)
            cb = _dot_nt(cg, bg)
            cb = jnp.concatenate([cb] * per_group, axis=1)
            cb = jnp.concatenate([cb] * per_group, axis=0)
            x_rows = jnp.concatenate([xg[:, hl * hd:(hl + 1) * hd] for hl in range(per_group)], axis=0)
            intra = _dot(cb * lmat * dt_row, x_rows)
            wdec = jnp.concatenate(
                [jnp.broadcast_to(jnp.exp(al - ac_c[r, SM_DT + h:SM_DT + h + 1]) * dt_c[r, SM_DT + h:SM_DT + h + 1],
                                  (CHUNK, hd)) for al, h in zip(a_last, heads)], axis=1)
            pre[c, g] = dict(
                intra=intra, ce=jnp.concatenate([cg] * per_group, axis=0) * jnp.exp(acol),
                upd=_dot_tn(xg * wdec, bg),
                keep=jnp.concatenate([jnp.broadcast_to(jnp.exp(al), (hd, 1)) for al in a_last], axis=0))
    for c in range(tm // CHUNK):
        r = slice(c * CHUNK, (c + 1) * CHUNK)
        for g in range(SSM_GROUPS):
            p = pre[c, g]
            hg = h_scr[g * per_group:(g + 1) * per_group].reshape(per_group * hd, SSM_STATE)
            from_state = _dot_nt(p["ce"], hg)
            for hl in range(per_group):
                h = g * per_group + hl
                rows, cols = slice(hl * CHUNK, (hl + 1) * CHUNK), slice(hl * hd, (hl + 1) * hd)
                y_scr[r, h * hd:(h + 1) * hd] = p["intra"][rows] + from_state[rows, cols]
            h_scr[g * per_group:(g + 1) * per_group] = (hg * p["keep"] + p["upd"]).reshape(per_group, hd, SSM_STATE)
    y = (y_scr[...] + d_ref[...] * xbc[:, 0:SSM_W]) * _silu(z_ref[0])
    o_ref[0] = _rms(y, ng_ref[...])
    hfin_ref[0] = h_scr[...]


def _ssd_prompt(xbc, z, small, small_t, buf0, h0, prm, tm=256):
    b, s, _ = xbc.shape
    tm = min(tm, s)
    nt = s // tm
    vec = lambda shape: pl.BlockSpec(shape, lambda bi, t: (0,) * len(shape))
    return pl.pallas_call(
        functools.partial(_ssd_prompt_kernel, tm=tm),
        grid=(b, nt),
        in_specs=[pl.BlockSpec((1, tm, SSM_CONV_CH), lambda bi, t: (bi, t, 0)),
                  pl.BlockSpec((1, tm, SSM_W), lambda bi, t: (bi, t, 0)),
                  pl.BlockSpec((1, tm, LANES), lambda bi, t: (bi, t, 0)),
                  pl.BlockSpec((LANES, tm), lambda bi, t: (0, bi * nt + t)),
                  pl.BlockSpec((1, SSM_CONV - 1, SSM_CONV_CH), lambda bi, t: (bi, 0, 0)),
                  pl.BlockSpec((1, SSM_HEADS, SSM_HEAD_DIM, SSM_STATE), lambda bi, t: (bi, 0, 0, 0)),
                  vec((SSM_CONV, SSM_CONV_CH)), vec((1, SSM_CONV_CH)), vec((1, LANES)), vec((1, LANES)),
                  vec((SSM_HEADS, 1)), vec((SSM_HEADS, 1)), vec((1, SSM_W)), vec((1, SSM_W))],
        out_specs=[pl.BlockSpec((1, tm, SSM_W), lambda bi, t: (bi, t, 0)),
                   pl.BlockSpec((1, SSM_HEADS, SSM_HEAD_DIM, SSM_STATE), lambda bi, t: (bi, 0, 0, 0))],
        out_shape=[jax.ShapeDtypeStruct((b, s, SSM_W), F32),
                   jax.ShapeDtypeStruct((b, SSM_HEADS, SSM_HEAD_DIM, SSM_STATE), F32)],
        scratch_shapes=[pltpu.VMEM((SUBLANES + tm, SSM_CONV_CH), F32),
                        pltpu.VMEM((SSM_HEADS, SSM_HEAD_DIM, SSM_STATE), F32),
                        pltpu.VMEM((tm, SSM_W), F32)],
        compiler_params=_cparams("parallel", "arbitrary"), name="ssd_prompt",
    )(xbc, z, small, small_t, buf0, h0, prm["ssm_cw"], prm["ssm_cb"], prm["ssm_alog_r"], prm["ssm_dtb_r"],
      prm["ssm_alog_c"], prm["ssm_dtb_c"], prm["ssm_d"], prm["ssm_ng"])


def _ssd_sample_kernel(x_ref, b0_ref, b1_ref, b2_ref, sm_ref, z_ref, h_ref, cw_ref, cb_ref, alog_ref, dtb_ref,
                       d_ref, ng_ref, rx_ref, rxt_ref, o_ref, hnew_ref, y_scr):
    cw = cw_ref[...]
    xbc = _silu(b0_ref[...] * cw[0:1] + b1_ref[...] * cw[1:2] + b2_ref[...] * cw[2:3] + x_ref[...] * cw[3:4]
                + cb_ref[...])
    gn = SSM_GROUPS * SSM_STATE
    dt_all = _softplus(sm_ref[...] + dtb_ref[...])
    da_all = jnp.exp(dt_all * -jnp.exp(alog_ref[...]))
    sz = SSM_HEAD_DIM * SSM_STATE
    per_group = SSM_HEADS // SSM_GROUPS
    for h in range(SSM_HEADS):
        g = h // per_group
        hs = slice(h * SSM_HEAD_DIM, (h + 1) * SSM_HEAD_DIM)
        xh = xbc[:, hs]
        bg = xbc[:, SSM_W + g * SSM_STATE:SSM_W + (g + 1) * SSM_STATE]
        cg = xbc[:, SSM_W + gn + g * SSM_STATE:SSM_W + gn + (g + 1) * SSM_STATE]
        dt = dt_all[:, SM_DT + h:SM_DT + h + 1]
        da = da_all[:, SM_DT + h:SM_DT + h + 1]
        hn = (h_ref[:, h * sz:(h + 1) * sz] * da
              + _dot_sel(xh * dt, rx_ref[...]) * jnp.concatenate([bg] * SSM_HEAD_DIM, axis=1))
        hnew_ref[:, h * sz:(h + 1) * sz] = hn
        y_scr[:, hs] = _dot_sel(hn * jnp.concatenate([cg] * SSM_HEAD_DIM, axis=1), rxt_ref[...])
    y = (y_scr[...] + d_ref[...] * xbc[:, 0:SSM_W]) * _silu(z_ref[...])
    o_ref[...] = _rms(y, ng_ref[...])


def _ssd_sample(xbc, buf, small, z, state, prm, tb=16):
    b = xbc.shape[0]
    sz = SSM_HEADS * SSM_HEAD_DIM * SSM_STATE
    rx, _ = _expand_mats(SSM_HEAD_DIM, SSM_STATE)
    row = lambda w: pl.BlockSpec((tb, w), lambda i: (i, 0))
    vec = lambda shape: pl.BlockSpec(shape, lambda i: (0,) * len(shape))
    o, h_new = pl.pallas_call(
        _ssd_sample_kernel, grid=(b // tb,),
        in_specs=[row(SSM_CONV_CH)] * 4 + [row(LANES), row(SSM_W), row(sz),
                  vec((SSM_CONV, SSM_CONV_CH)), vec((1, SSM_CONV_CH)), vec((1, LANES)), vec((1, LANES)),
                  vec((1, SSM_W)), vec((1, SSM_W)), vec(rx.shape), vec(rx.T.shape)],
        out_specs=[row(SSM_W), row(sz)],
        out_shape=[jax.ShapeDtypeStruct((b, SSM_W), F32), jax.ShapeDtypeStruct((b, sz), F32)],
        scratch_shapes=[pltpu.VMEM((tb, SSM_W), F32)],
        compiler_params=_cparams("parallel"), name="ssd_sample",
    )(xbc, buf[:, 0], buf[:, 1], buf[:, 2], small, z, state.reshape(b, sz),
      prm["ssm_cw"], prm["ssm_cb"], prm["ssm_alog_r"], prm["ssm_dtb_r"], prm["ssm_d"], prm["ssm_ng"], rx, rx.T)
    return o, h_new.reshape(state.shape)


def _xattn_prompt_kernel(x_ref, gpre_ref, wq_ref, kt_ref, v_ref, wo_ref, gpost_ref, o_ref, att_scr):
    x = x_ref[0]
    q = jnp.dot(_rms(x, gpre_ref[...]).astype(BF16), wq_ref[...], preferred_element_type=F32)
    q = (q * X_HEAD_DIM ** -0.5).astype(BF16)
    for h in range(X_HEADS):
        hs = slice(h * X_HEAD_DIM, (h + 1) * X_HEAD_DIM)
        lg = jnp.dot(q[:, hs], kt_ref[0, hs, :], preferred_element_type=F32)
        p = jnp.exp(lg - jnp.max(lg, axis=-1, keepdims=True))
        p = p / jnp.sum(p, axis=-1, keepdims=True)
        att_scr[:, hs] = jnp.dot(p.astype(BF16), v_ref[0, :, hs], preferred_element_type=F32)
    y = jnp.dot(att_scr[...].astype(BF16), wo_ref[...], preferred_element_type=F32)
    o_ref[0] = x + _rms(y, gpost_ref[...])


def _xattn_prompt(x, gpre, wq, mem_k, mem_v, wo, gpost, tm=512):
    b, s, d = x.shape
    tm = min(tm, s)
    kt = mem_k.transpose(0, 2, 1).astype(BF16)
    vb = mem_v.astype(BF16)
    vec = lambda shape: pl.BlockSpec(shape, lambda bi, t: (0,) * len(shape))
    return pl.pallas_call(
        _xattn_prompt_kernel, grid=(b, s // tm),
        in_specs=[pl.BlockSpec((1, tm, d), lambda bi, t: (bi, t, 0)), vec((1, d)), vec(wq.shape),
                  pl.BlockSpec((1, X_W, MEM_LEN), lambda bi, t: (bi, 0, 0)),
                  pl.BlockSpec((1, MEM_LEN, X_W), lambda bi, t: (bi, 0, 0)), vec(wo.shape), vec((1, d))],
        out_specs=pl.BlockSpec((1, tm, d), lambda bi, t: (bi, t, 0)),
        out_shape=jax.ShapeDtypeStruct((b, s, d), F32),
        scratch_shapes=[pltpu.VMEM((tm, X_W), F32)],
        compiler_params=_cparams("parallel", "parallel"), name="xattn_prompt",
    )(x, gpre.reshape(1, d), wq, kt, vb, wo, gpost.reshape(1, d))


def _xattn_sample_kernel(qt_ref, k_ref, v_ref, ot_ref, *, tb):
    i = pl.program_id(0)

    @pl.when(i == 0)
    def _():
        ot_ref[...] = jnp.zeros(ot_ref.shape, F32)

    lane = lax.broadcasted_iota(I32, ot_ref.shape, 1)
    for j in range(tb):
        b = i * tb + j
        q = _lane_column(qt_ref, b) * X_HEAD_DIM ** -0.5
        out = []
        for h in range(X_HEADS):
            rows = slice(h * X_HEAD_DIM, (h + 1) * X_HEAD_DIM)
            lg = jnp.sum(k_ref[0, j, h] * q[rows], axis=0, keepdims=True)
            e = jnp.exp(lg - jnp.max(lg, axis=1, keepdims=True))
            pn = e / jnp.sum(e, axis=1, keepdims=True)
            out.append(jnp.sum(v_ref[0, j, h] * pn, axis=1, keepdims=True))
        ot_ref[...] = jnp.where(lane == b, jnp.concatenate(out, axis=0), ot_ref[...])


def _xattn_sample(l, q, cache_mem_k, cache_mem_v, tb=8):
    b = q.shape[0]
    assert b == LANES
    mkt = cache_mem_k.transpose(0, 1, 3, 4, 2)
    mvt = cache_mem_v.transpose(0, 1, 3, 4, 2)
    mem = pl.BlockSpec((1, tb, X_HEADS, X_HEAD_DIM, MEM_LEN), lambda i: (l, i, 0, 0, 0))
    resident = pl.BlockSpec((X_W, LANES), lambda i: (0, 0))
    ot = pl.pallas_call(
        functools.partial(_xattn_sample_kernel, tb=tb), grid=(b // tb,),
        in_specs=[resident, mem, mem], out_specs=resident,
        out_shape=jax.ShapeDtypeStruct((X_W, b), F32),
        compiler_params=_cparams("arbitrary"), name="xattn_sample",
    )(q.T, mkt, mvt)
    return ot.T


def _ffn_prompt_kernel(x_ref, g_ref, wg_ref, wu_ref, cw_ref, buf0_ref, h_ref, tail_ref, cbuf, *, tm):
    t = pl.program_id(1)

    @pl.when(t == 0)
    def _():
        cbuf[SUBLANES - (FFN_CONV - 1):SUBLANES, :] = buf0_ref[0]

    hn = _rms(x_ref[0], g_ref[...]).astype(BF16)
    gate = jnp.dot(hn, wg_ref[...], preferred_element_type=F32)
    tail_ref[0] = gate[tm - SUBLANES:tm, :]
    conv = _conv_tile(cbuf, gate, cw_ref[...], FFN_CONV, tm)
    up = jnp.dot(hn, wu_ref[...], preferred_element_type=F32)
    h_ref[0] = (_silu(conv) * up).astype(h_ref.dtype)


def _ffn_prompt(x, g, wg, wu, cw, buf0, tm=256):
    b, s, d = x.shape
    tm = min(tm, s)
    f = wg.shape[1]
    vec = lambda shape: pl.BlockSpec(shape, lambda bi, t: (0,) * len(shape))
    return pl.pallas_call(
        functools.partial(_ffn_prompt_kernel, tm=tm), grid=(b, s // tm),
        in_specs=[pl.BlockSpec((1, tm, d), lambda bi, t: (bi, t, 0)), vec((1, d)), vec(wg.shape), vec(wu.shape),
                  vec((FFN_CONV, f)), pl.BlockSpec((1, FFN_CONV - 1, f), lambda bi, t: (bi, 0, 0))],
        out_specs=[pl.BlockSpec((1, tm, f), lambda bi, t: (bi, t, 0)),
                   pl.BlockSpec((1, SUBLANES, f), lambda bi, t: (bi, 0, 0))],
        out_shape=[jax.ShapeDtypeStruct((b, s, f), BF16), jax.ShapeDtypeStruct((b, SUBLANES, f), F32)],
        scratch_shapes=[pltpu.VMEM((SUBLANES + tm, f), F32)],
        compiler_params=_cparams("parallel", "arbitrary"), name="ffn_prompt",
    )(x, g.reshape(1, d), wg, wu, cw, buf0)


def _ffn_sample_kernel(x_ref, g_ref, wg_ref, wu_ref, cw_ref, b0_ref, b1_ref, h_ref, gate_ref):
    hn = _rms(x_ref[...], g_ref[...]).astype(BF16)
    gate = jnp.dot(hn, wg_ref[...], preferred_element_type=F32)
    gate_ref[...] = gate
    cw = cw_ref[...]
    conv = b0_ref[...] * cw[0:1] + b1_ref[...] * cw[1:2] + gate * cw[2:3]
    h_ref[...] = (_silu(conv) * jnp.dot(hn, wu_ref[...], preferred_element_type=F32)).astype(h_ref.dtype)


def _ffn_sample(x, g, wg, wu, cw, buf, n_split=2):
    b, d = x.shape
    f = wg.shape[1]
    tn = f // n_split
    col = lambda rows: pl.BlockSpec((rows, tn), lambda j: (0, j))
    return pl.pallas_call(
        _ffn_sample_kernel, grid=(n_split,),
        in_specs=[pl.BlockSpec((b, d), lambda j: (0, 0)), pl.BlockSpec((1, d), lambda j: (0, 0)),
                  col(d), col(d), col(FFN_CONV), col(b), col(b)],
        out_specs=[col(b), col(b)],
        out_shape=[jax.ShapeDtypeStruct((b, f), BF16), jax.ShapeDtypeStruct((b, f), F32)],
        compiler_params=_cparams("parallel"), name="ffn_sample",
    )(x, g.reshape(1, d), wg, wu, cw, buf[:, 0], buf[:, 1])


def _lane_row(vals, start):
    return jnp.zeros((1, LANES), F32).at[0, start:start + vals.shape[0]].set(vals.astype(F32))


def _layer_params(l, P):
    w_in = P["w_in"][l]
    cuts = np.concatenate([[0], np.cumsum(IN_SIZES)])
    seg = lambda i: w_in[:, cuts[i]:cuts[i + 1]]
    w_small = jnp.concatenate([seg(4), seg(5), seg(7), seg(8), seg(12),
                               jnp.zeros((D_MODEL, LANES - SM_USED), F32)], axis=1)
    bf = lambda a: a.astype(BF16)
    w_out = P["w_out"][l]
    dn_alog, dn_dtb = P["dn_a_log"][l], P["dn_dt_bias"][l]
    ssm_alog, ssm_dtb = P["ssm_a_log"][l], P["ssm_dt_bias"][l]
    col8 = lambda v, start: jnp.zeros((SUBLANES, 1), F32).at[start:start + v.shape[0], 0].set(v.astype(F32))
    return dict(
        g_pre_mix=P["g_pre_mix"][l],
        w_in=[bf(w_in[:, :cuts[4]]), bf(w_small), bf(seg(6)), bf(seg(9)), bf(seg(10)), bf(seg(11))],
        w_small_t=bf(w_small.T),
        dn_cw=P["dn_conv_w"][l], dn_ng=P["dn_norm_g"][l].reshape(1, DN_HEAD_DIM),
        dn_alog_r=_lane_row(dn_alog, SM_ALPHA), dn_dtb_r=_lane_row(dn_dtb, SM_ALPHA),
        dn_alog_c=col8(dn_alog, DN_HEADS), dn_dtb_c=col8(dn_dtb, DN_HEADS),
        ssm_cw=P["ssm_conv_w"][l], ssm_cb=P["ssm_conv_b"][l].reshape(1, SSM_CONV_CH),
        ssm_alog_r=_lane_row(ssm_alog, SM_DT), ssm_dtb_r=_lane_row(ssm_dtb, SM_DT),
        ssm_alog_c=ssm_alog.reshape(SSM_HEADS, 1), ssm_dtb_c=ssm_dtb.reshape(SSM_HEADS, 1),
        ssm_d=jnp.repeat(P["ssm_d"][l], SSM_HEAD_DIM).reshape(1, SSM_W),
        ssm_ng=P["ssm_norm_g"][l].reshape(1, SSM_W),
        w_out=[bf(w_out[:A_W]), bf(w_out[A_W:A_W + DN_W]), bf(w_out[A_W + DN_W:])],
        g_post_mix=P["g_post_mix"][l], g_pre_x=P["g_pre_x"][l], w_xq=bf(P["w_xq"][l]),
        w_xk=bf(P["w_xk"][l]), w_xv=bf(P["w_xv"][l]), w_xo=bf(P["w_xo"][l]), g_post_x=P["g_post_x"][l],
        g_pre_ffn=P["g_pre_ffn"][l], w_gate=bf(P["w_gate"][l]), w_up=bf(P["w_up"][l]),
        ffn_cw=P["ffn_conv_w"][l], w_down=bf(P["w_down"][l]), g_post_ffn=P["g_post_ffn"][l],
    )


def _layer_prompt(x, prm, mem):
    b, s, d = x.shape
    t = b * s
    x2 = x.reshape(t, d)
    a4, small, dqkv, dgate, sz, sxbc, small_t = _proj(x2, prm["g_pre_mix"], prm["w_in"], wt=prm["w_small_t"])
    a4 = a4.reshape(b, s, 4 * A_W)
    small3 = small.reshape(b, s, LANES)
    ak, av = a4[..., A_W:2 * A_W], a4[..., 2 * A_W:3 * A_W]
    aki = small3[..., SM_KI:SM_KI + IDX_DIM]
    o_a = _dsa_prompt(a4, small3)
    dqkv3 = dqkv.reshape(b, s, 3 * DN_W)
    o_b, dn_s = _dn_prompt(dqkv3, dgate.reshape(b, s, DN_W), small3, small_t,
                           jnp.zeros((b, DN_CONV - 1, 3 * DN_W), F32),
                           jnp.zeros((b, DN_HEADS, DN_HEAD_DIM, DN_HEAD_DIM), F32), prm)
    sxbc3 = sxbc.reshape(b, s, SSM_CONV_CH)
    o_c, ssm_h = _ssd_prompt(sxbc3, sz.reshape(b, s, SSM_W), small3, small_t,
                             jnp.zeros((b, SSM_CONV - 1, SSM_CONV_CH), F32),
                             jnp.zeros((b, SSM_HEADS, SSM_HEAD_DIM, SSM_STATE), F32), prm)
    x2 = _mm_post(x2, prm["g_post_mix"], [o_a.reshape(t, A_W), o_b.reshape(t, DN_W), o_c.reshape(t, SSM_W)],
                  prm["w_out"])
    mk, mv = _proj(mem.reshape(-1, d), None, [prm["w_xk"], prm["w_xv"]])
    mk, mv = mk.reshape(b, MEM_LEN, X_W), mv.reshape(b, MEM_LEN, X_W)
    x3 = _xattn_prompt(x2.reshape(b, s, d), prm["g_pre_x"], prm["w_xq"], mk, mv, prm["w_xo"], prm["g_post_x"])
    h, tail = _ffn_prompt(x3, prm["g_pre_ffn"], prm["w_gate"], prm["w_up"], prm["ffn_cw"],
                          jnp.zeros((b, FFN_CONV - 1, D_FF), F32))
    x2 = _mm_post(x3.reshape(t, d), prm["g_post_ffn"], [h.reshape(t, D_FF)], [prm["w_down"]])
    state = (ak.reshape(b, s, A_HEADS, A_HEAD_DIM), av.reshape(b, s, A_HEADS, A_HEAD_DIM), aki,
             dqkv3[:, s - (DN_CONV - 1):], dn_s, sxbc3[:, s - (SSM_CONV - 1):], ssm_h,
             tail[:, SUBLANES - (FFN_CONV - 1):],
             mk.reshape(b, MEM_LEN, X_HEADS, X_HEAD_DIM), mv.reshape(b, MEM_LEN, X_HEADS, X_HEAD_DIM))
    return x2.reshape(b, s, d), state


def _layer_sample(x, l, prm, caches):
    (cache_k, cache_v, cache_idx_k, page_table, cache_mem_k, cache_mem_v,
     dn_buf, dn_s, ssm_buf, ssm_h, ffn_buf) = caches
    b, d = x.shape
    a4, small, dqkv, dgate, sz, sxbc = _proj(x, prm["g_pre_mix"], prm["w_in"], tm=b)
    ak, av = a4[:, A_W:2 * A_W], a4[:, 2 * A_W:3 * A_W]
    aki = small[:, SM_KI:SM_KI + IDX_DIM]
    o_a = _dsa_sample(l, a4[:, :A_W], ak, av, a4[:, 3 * A_W:], aki, small[:, SM_WI:SM_WI + IDX_HEADS],
                      cache_k, cache_v, cache_idx_k, page_table)
    o_b, dn_s = _dn_sample(dqkv, dn_buf, small, dgate, dn_s, prm)
    o_c, ssm_h = _ssd_sample(sxbc, ssm_buf, small, sz, ssm_h, prm)
    x = _mm_post(x, prm["g_post_mix"], [o_a, o_b, o_c], prm["w_out"], tm=b)
    (xq,) = _proj(x, prm["g_pre_x"], [prm["w_xq"]], tm=b)
    att = _xattn_sample(l, xq, cache_mem_k, cache_mem_v)
    x = _mm_post(x, prm["g_post_x"], [att], [prm["w_xo"]], tm=b)
    h, gate = _ffn_sample(x, prm["g_pre_ffn"], prm["w_gate"], prm["w_up"], prm["ffn_cw"], ffn_buf)
    x = _mm_post(x, prm["g_post_ffn"], [h], [prm["w_down"]], tm=b)
    state = (ak.reshape(b, 1, A_HEADS, A_HEAD_DIM), av.reshape(b, 1, A_HEADS, A_HEAD_DIM), aki.reshape(b, 1, IDX_DIM),
             jnp.concatenate([dn_buf[:, 1:], dqkv[:, None]], axis=1), dn_s,
             jnp.concatenate([ssm_buf[:, 1:], sxbc[:, None]], axis=1), ssm_h,
             jnp.concatenate([ffn_buf[:, 1:], gate[:, None]], axis=1))
    return x, state


def kernel(x_prompt, x_sample, cache_k, cache_v, cache_idx_k, state_dn_conv, state_dn, state_ssm_conv, state_ssm,
           state_ffn_conv, cache_mem_k, cache_mem_v, page_table, mem_prompt,
           g_pre_mix, w_in, dn_conv_w, dn_a_log, dn_dt_bias, dn_norm_g, ssm_conv_w, ssm_conv_b, ssm_a_log,
           ssm_dt_bias, ssm_d, ssm_norm_g, w_out, g_post_mix, g_pre_x, w_xq, w_xk, w_xv, w_xo, g_post_x,
           g_pre_ffn, w_gate, w_up, ffn_conv_w, w_down, g_post_ffn):
    P = dict(g_pre_mix=g_pre_mix, w_in=w_in, dn_conv_w=dn_conv_w, dn_a_log=dn_a_log, dn_dt_bias=dn_dt_bias,
             dn_norm_g=dn_norm_g, ssm_conv_w=ssm_conv_w, ssm_conv_b=ssm_conv_b, ssm_a_log=ssm_a_log,
             ssm_dt_bias=ssm_dt_bias, ssm_d=ssm_d, ssm_norm_g=ssm_norm_g, w_out=w_out, g_post_mix=g_post_mix,
             g_pre_x=g_pre_x, w_xq=w_xq, w_xk=w_xk, w_xv=w_xv, w_xo=w_xo, g_post_x=g_post_x,
             g_pre_ffn=g_pre_ffn, w_gate=w_gate, w_up=w_up, ffn_conv_w=ffn_conv_w, w_down=w_down,
             g_post_ffn=g_post_ffn)
    depth = w_in.shape[0]
    params = [_layer_params(l, P) for l in range(depth)]

    x, ps = x_prompt, []
    for l in range(depth):
        x, st = _layer_prompt(x, params[l], mem_prompt)
        ps.append(st)
    y_prompt = x

    x, ss = x_sample.reshape(x_sample.shape[0], D_MODEL), []
    for l in range(depth):
        caches = (cache_k, cache_v, cache_idx_k, page_table, cache_mem_k, cache_mem_v,
                  state_dn_conv[l], state_dn[l], state_ssm_conv[l], state_ssm[l], state_ffn_conv[l])
        x, st = _layer_sample(x, l, params[l], caches)
        ss.append(st)
    y_sample = x.reshape(x_sample.shape)

    stk = lambda states, i: jnp.stack([st[i] for st in states])
    return (y_prompt, y_sample,
            stk(ps, 0), stk(ps, 1), stk(ps, 2), stk(ss, 0), stk(ss, 1), stk(ss, 2),
            stk(ps, 3), stk(ps, 4), stk(ss, 3), stk(ss, 4),
            stk(ps, 5), stk(ps, 6), stk(ss, 5), stk(ss, 6),
            stk(ps, 7), stk(ss, 7),
            stk(ps, 8), stk(ps, 9))
```
